```python
import math
import jax, jax.numpy as jnp
from jax import lax
import numpy as np

D_MODEL = 2048
BATCH = 16
SEQ = 256
DEPTH = 2
DEC_BATCH = 8
DEC_SEQ = 2048
PAST_LEN = 512

GRID_W = 64
D_A = D_MODEL // 2
EMB_DIM = 33
N_BANDS = (EMB_DIM - 1) // 2
FILTER_FF = 64
DECAY_TARGET = 1e-2
FAST_DECAY_PCT = 0.3
SLOW_DECAY_PCT = 1.5
MAX_DECAY = math.log(DECAY_TARGET) / FAST_DECAY_PCT
MIN_DECAY = math.log(DECAY_TARGET) / SLOW_DECAY_PCT
N_HEADS = 8
QK_NOPE = 128
ROPE_DIM = 64
V_DIM = 128
Q_LORA = D_MODEL // 4
KV_LORA = D_MODEL // 8
D_B = N_HEADS * V_DIM
ROPE_THETA = 10000.0
ROPE_HALF = ROPE_DIM // 2
AXIS_PAIRS = ROPE_HALF // 2
ATTN_SCALE = (QK_NOPE + ROPE_DIM) ** -0.5
Q_BLOCK = 128
D_C = D_MODEL // 2
D_FF = 5632
EPS = 1e-6
N_IN = 3 * D_A + Q_LORA + KV_LORA + ROPE_DIM + 3 * D_C + 3 * D_MODEL
SPLIT_IDX = (3 * D_A,
             3 * D_A + Q_LORA,
             3 * D_A + Q_LORA + KV_LORA,
             3 * D_A + Q_LORA + KV_LORA + ROPE_DIM,
             3 * D_A + Q_LORA + KV_LORA + ROPE_DIM + 3 * D_C)

kernel_name = "hybrid_flow_prefix_hyena_mla_shortconv"


def rmsnorm(x, g):
    xf = x.astype(jnp.float32)
    y = xf * lax.rsqrt(jnp.mean(xf * xf, axis=-1, keepdims=True) + EPS)
    return y.astype(x.dtype) * g


def dwconv3(x, w):
    xp = jnp.pad(x, ((0, 0), (1, 1), (0, 0)))
    return xp[:, :-2] * w[0] + x * w[1] + xp[:, 2:] * w[2]


def axial_rope(L):
    rows = L // GRID_W
    row = jnp.repeat(jnp.arange(rows, dtype=jnp.float32), GRID_W)
    col = jnp.tile(jnp.arange(GRID_W, dtype=jnp.float32), rows)
    inv = ROPE_THETA ** (-jnp.arange(AXIS_PAIRS, dtype=jnp.float32) / AXIS_PAIRS)
    ang = jnp.concatenate([row[:, None] * inv, col[:, None] * inv], axis=-1)
    return jnp.cos(ang), jnp.sin(ang)


def apply_rope(x, cos, sin):
    cos = cos.astype(x.dtype)
    sin = sin.astype(x.dtype)
    x1, x2 = x[..., :ROPE_HALF], x[..., ROPE_HALF:]
    return jnp.concatenate([x1 * cos - x2 * sin, x1 * sin + x2 * cos], axis=-1)


def hyena_filter(L, lp):
    f32 = jnp.float32
    t_idx = jnp.arange(L, dtype=f32)
    t_norm = t_idx / max(L - 1, 1)
    w = 2.0 * math.pi * t_idx / L
    bands = jnp.linspace(1e-4, N_BANDS - 1, N_BANDS, dtype=f32)
    ang = w[:, None] * bands[None, :]
    z = jnp.concatenate([t_norm[:, None], jnp.cos(ang), -jnp.sin(ang)], axis=-1)
    freq = lp['hy_f_freq'].astype(f32)
    hdn = jnp.sin(freq[0] * (z @ lp['hy_f_w1'].astype(f32) + lp['hy_f_b1'].astype(f32)))
    hdn = jnp.sin(freq[1] * (hdn @ lp['hy_f_w2'].astype(f32) + lp['hy_f_b2'].astype(f32)))
    hf = hdn @ lp['hy_f_w3'].astype(f32) + lp['hy_f_b3'].astype(f32)
    deltas = jnp.abs(jnp.linspace(MIN_DECAY, MAX_DECAY, D_A, dtype=f32))
    window = jnp.exp(-t_norm[:, None] * deltas[None, :])
    hf = hf * jnp.concatenate([window, window], axis=-1)
    h_fwd, h_bwd = hf[:, :D_A], hf[:, D_A:]
    k = jnp.concatenate([h_fwd, jnp.zeros((1, D_A), f32), h_bwd[:0:-1]], axis=0)
    return k / jnp.sum(jnp.abs(k), axis=0, keepdims=True)


def long_conv(u, k):
    L = u.shape[1]
    uf = jnp.fft.rfft(u.astype(jnp.float32), n=2 * L, axis=1)
    kf = jnp.fft.rfft(k, n=2 * L, axis=0)
    y = jnp.fft.irfft(uf * kf[None], n=2 * L, axis=1)[:, :L]
    return y.astype(u.dtype)


def mla_attend(q_lat, q_pe, keys_c, keys_pe):
    B, Lq, H, C = q_lat.shape
    nb = Lq // Q_BLOCK
    qlb = q_lat.reshape(B, nb, Q_BLOCK, H, C).swapaxes(0, 1)
    qpb = q_pe.reshape(B, nb, Q_BLOCK, H, ROPE_DIM).swapaxes(0, 1)

    def block(args):
        ql, qp = args
        s = (jnp.einsum('bqhc,bkc->bhqk', ql, keys_c)
             + jnp.einsum('bqhr,bkr->bhqk', qp, keys_pe)).astype(jnp.float32) * ATTN_SCALE
        p = jax.nn.softmax(s, axis=-1).astype(keys_c.dtype)
        return jnp.einsum('bhqk,bkc->bqhc', p, keys_c)

    out = lax.map(block, (qlb, qpb))
    return out.swapaxes(0, 1).reshape(B, Lq, H, C)


def token_mix(h, lp, rope, ctx_kv):
    Bsz, L, _ = h.shape
    proj = h @ lp['w_in']
    hy_in, cq, ckv, kpe, sc_in, gates = jnp.split(proj, SPLIT_IDX, axis=-1)
    hy = dwconv3(hy_in, lp['hy_conv_w']) + lp['hy_conv_b']
    x0, x1, v = jnp.split(hy, 3, axis=-1)
    zz = x1 * v
    y_a = x0 * (long_conv(zz, hyena_filter(L, lp)) + zz * lp['hy_bias'])
    q = (rmsnorm(cq, lp['q_norm']) @ lp['w_uq']).reshape(Bsz, L, N_HEADS, QK_NOPE + ROPE_DIM)
    q_nope, q_pe = q[..., :QK_NOPE], q[..., QK_NOPE:]
    ckv_n = rmsnorm(ckv, lp['kv_norm'])
    if rope is not None:
        cos, sin = rope
        q_pe = apply_rope(q_pe, cos[:, None, :], sin[:, None, :])
        kpe_r = apply_rope(kpe, cos, sin)
    else:
        kpe_r = kpe
    if ctx_kv is not None:
        keys_c = jnp.concatenate([ckv_n, ctx_kv[0]], axis=1)
        keys_pe = jnp.concatenate([kpe_r, ctx_kv[1]], axis=1)
    else:
        keys_c, keys_pe = ckv_n, kpe_r
    w_ukv = lp['w_ukv'].reshape(KV_LORA, N_HEADS, QK_NOPE + V_DIM)
    q_lat = jnp.einsum('blhn,chn->blhc', q_nope, w_ukv[..., :QK_NOPE])
    o_lat = mla_attend(q_lat, q_pe, keys_c, keys_pe)
    y_b = jnp.einsum('blhc,chv->blhv', o_lat, w_ukv[..., QK_NOPE:]).reshape(Bsz, L, D_B)
    b_g, c_g, u = jnp.split(sc_in, 3, axis=-1)
    y_c = b_g * dwconv3(c_g * u, lp['sc_conv_w'])
    g_a, g_b, g_c = jnp.split(gates, 3, axis=-1)
    m = (jax.nn.sigmoid(g_a) * (y_a @ lp['w_br_a'])
         + jax.nn.sigmoid(g_b) * (y_b @ lp['w_br_b'])
         + jax.nn.sigmoid(g_c) * (y_c @ lp['w_br_c']))
    return m @ lp['w_o'], (ckv_n, kpe)


def conv_ffn(h, lp):
    uu = dwconv3(h @ lp['ffn_up'], lp['ffn_conv_w']) + lp['ffn_conv_b']
    g, v = jnp.split(uu, 2, axis=-1)
    return (jax.nn.silu(g) * v) @ lp['ffn_down']


def trunk_layer(x, cvec, lp, rope, ctx_kv):
    mod = jax.nn.silu(cvec) @ lp['ada_w'] + lp['ada_b']
    sh1, sc1, g1, sh2, sc2, g2 = jnp.split(mod[:, None, :], 6, axis=-1)
    h = rmsnorm(x, lp['norm_mix_pre']) * (1.0 + sc1) + sh1
    o, kv = token_mix(h, lp, rope, ctx_kv)
    x = x + g1 * rmsnorm(o, lp['norm_mix_post'])
    h = rmsnorm(x, lp['norm_ffn_pre']) * (1.0 + sc2) + sh2
    x = x + g2 * rmsnorm(conv_ffn(h, lp), lp['norm_ffn_post'])
    return x, kv


def setup_inputs(seed: int = 0) -> dict:
    key = jax.random.key(seed)
    ks = iter(jax.random.split(key, 64))

    def nrm(shape, scale):
        return jax.random.normal(next(ks), shape, jnp.float32) * scale

    def gain(shape):
        return 1.0 + nrm(shape, 0.05)

    return {
        'x_prompt': nrm((BATCH, SEQ, D_MODEL), 1.0),
        'x_sample': nrm((DEC_BATCH, DEC_SEQ, D_MODEL), 1.0),
        'c': nrm((DEC_BATCH, D_MODEL), 1.0),
        'cache_ckv': nrm((DEC_BATCH, DEPTH, PAST_LEN, KV_LORA), 1.0),
        'cache_kpe': nrm((DEC_BATCH, DEPTH, PAST_LEN, ROPE_DIM), 1.0),
        'c_ctx': nrm((D_MODEL,), 1.0),
        'ada_w': nrm((DEPTH, D_MODEL, 6 * D_MODEL), 0.5 * D_MODEL ** -0.5),
        'ada_b': nrm((DEPTH, 6 * D_MODEL), 0.01),
        'norm_mix_pre': gain((DEPTH, D_MODEL)),
        'norm_mix_post': gain((DEPTH, D_MODEL)),
        'norm_ffn_pre': gain((DEPTH, D_MODEL)),
        'norm_ffn_post': gain((DEPTH, D_MODEL)),
        'w_in': nrm((DEPTH, D_MODEL, N_IN), D_MODEL ** -0.5),
        'hy_conv_w': nrm((DEPTH, 3, 3 * D_A), 3 ** -0.5),
        'hy_conv_b': nrm((DEPTH, 3 * D_A), 0.01),
        'hy_f_w1': nrm((DEPTH, EMB_DIM, FILTER_FF), EMB_DIM ** -0.5),
        'hy_f_b1': nrm((DEPTH, FILTER_FF), 0.01),
        'hy_f_w2': nrm((DEPTH, FILTER_FF, FILTER_FF), FILTER_FF ** -0.5),
        'hy_f_b2': nrm((DEPTH, FILTER_FF), 0.01),
        'hy_f_w3': nrm((DEPTH, FILTER_FF, 2 * D_A), FILTER_FF ** -0.5),
        'hy_f_b3': nrm((DEPTH, 2 * D_A), 0.01),
        'hy_f_freq': gain((DEPTH, 2, FILTER_FF)),
        'hy_bias': nrm((DEPTH, D_A), 0.1),
        'q_norm': gain((DEPTH, Q_LORA)),
        'kv_norm': gain((DEPTH, KV_LORA)),
        'w_uq': nrm((DEPTH, Q_LORA, N_HEADS * (QK_NOPE + ROPE_DIM)), Q_LORA ** -0.5),
        'w_ukv': nrm((DEPTH, KV_LORA, N_HEADS * (QK_NOPE + V_DIM)), KV_LORA ** -0.5),
        'sc_conv_w': nrm((DEPTH, 3, D_C), 3 ** -0.5),
        'w_br_a': nrm((DEPTH, D_A, D_MODEL), D_A ** -0.5),
        'w_br_b': nrm((DEPTH, D_B, D_MODEL), D_B ** -0.5),
        'w_br_c': nrm((DEPTH, D_C, D_MODEL), D_C ** -0.5),
        'w_o': nrm((DEPTH, D_MODEL, D_MODEL), D_MODEL ** -0.5),
        'ffn_up': nrm((DEPTH, D_MODEL, 2 * D_FF), D_MODEL ** -0.5),
        'ffn_conv_w': nrm((DEPTH, 3, 2 * D_FF), 3 ** -0.5),
        'ffn_conv_b': nrm((DEPTH, 2 * D_FF), 0.01),
        'ffn_down': nrm((DEPTH, D_FF, D_MODEL), D_FF ** -0.5),
    }


def reference(x_prompt, x_sample, c, cache_ckv, cache_kpe, c_ctx, ada_w, ada_b,
              norm_mix_pre, norm_mix_post, norm_ffn_pre, norm_ffn_post, w_in,
              hy_conv_w, hy_conv_b, hy_f_w1, hy_f_b1, hy_f_w2, hy_f_b2, hy_f_w3, hy_f_b3,
              hy_f_freq, hy_bias, q_norm, kv_norm, w_uq, w_ukv, sc_conv_w,
              w_br_a, w_br_b, w_br_c, w_o, ffn_up, ffn_conv_w, ffn_conv_b, ffn_down):
    rope = axial_rope(x_sample.shape[1])
    xp, xs = x_prompt, x_sample
    ckv_list, kpe_list = [], []
    for l in range(DEPTH):
        lp = dict(ada_w=ada_w[l], ada_b=ada_b[l],
                  norm_mix_pre=norm_mix_pre[l], norm_mix_post=norm_mix_post[l],
                  norm_ffn_pre=norm_ffn_pre[l], norm_ffn_post=norm_ffn_post[l],
                  w_in=w_in[l], hy_conv_w=hy_conv_w[l], hy_conv_b=hy_conv_b[l],
                  hy_f_w1=hy_f_w1[l], hy_f_b1=hy_f_b1[l], hy_f_w2=hy_f_w2[l], hy_f_b2=hy_f_b2[l],
                  hy_f_w3=hy_f_w3[l], hy_f_b3=hy_f_b3[l], hy_f_freq=hy_f_freq[l], hy_bias=hy_bias[l],
                  q_norm=q_norm[l], kv_norm=kv_norm[l], w_uq=w_uq[l], w_ukv=w_ukv[l],
                  sc_conv_w=sc_conv_w[l], w_br_a=w_br_a[l], w_br_b=w_br_b[l], w_br_c=w_br_c[l],
                  w_o=w_o[l], ffn_up=ffn_up[l], ffn_conv_w=ffn_conv_w[l], ffn_conv_b=ffn_conv_b[l],
                  ffn_down=ffn_down[l])
        xp, (ckv_l, kpe_l) = trunk_layer(xp, c_ctx[None, :], lp, None, None)
        ckv_list.append(ckv_l)
        kpe_list.append(kpe_l)
        xs, _ = trunk_layer(xs, c, lp, rope, (cache_ckv[:, l], cache_kpe[:, l]))
    new_ckv = jnp.stack(ckv_list, axis=1)
    new_kpe = jnp.stack(kpe_list, axis=1)
    return (xp, xs, new_ckv, new_kpe)
```

```python
import functools
import math

import jax
import jax.numpy as jnp
from jax import lax
from jax.experimental import pallas as pl
from jax.experimental.pallas import tpu as pltpu

F32 = jnp.float32
BF16 = jnp.bfloat16

D_MODEL = 2048
DEPTH = 2
GRID_W = 64
D_A = 1024
EMB_DIM = 33
N_BANDS = 16
FILTER_FF = 64
MAX_DECAY = math.log(1e-2) / 0.3
MIN_DECAY = math.log(1e-2) / 1.5
N_HEADS = 8
QK_NOPE = 128
ROPE_DIM = 64
V_DIM = 128
Q_LORA = 512
KV_LORA = 256
ROPE_HALF = 32
AXIS_PAIRS = 16
ROPE_THETA = 10000.0
ATTN_SCALE = (QK_NOPE + ROPE_DIM) ** -0.5
D_C = 1024
D_FF = 5632
EPS = 1e-6
FILTER_PAD = 128

VMEM_LIMIT_BYTES = 56 * 1024 * 1024
BF16_SUBLANES = 16


def _cparams(*sem):
    return pltpu.CompilerParams(dimension_semantics=sem, vmem_limit_bytes=VMEM_LIMIT_BYTES)


def _dot(a, b):
    return jnp.dot(a, b, preferred_element_type=F32)


def _dot_nt(a, b):
    return lax.dot_general(a, b, (((1,), (1,)), ((), ())), preferred_element_type=F32)


def _dot_hi(a, b):
    return jnp.dot(a, b, preferred_element_type=F32, precision=lax.Precision.HIGHEST)


def _rms(x):
    return x * lax.rsqrt(jnp.mean(x * x, axis=-1, keepdims=True) + EPS)


def _silu(x):
    return x * jax.nn.sigmoid(x)


def _conv3(u, w0, w1, w2, pos, seq_len, prev_row=None, next_row=None):
    tm = u.shape[0]
    up = pltpu.roll(u, 1, axis=0)
    dn = pltpu.roll(u, tm - 1, axis=0)
    if prev_row is not None:
        local = lax.broadcasted_iota(jnp.int32, u.shape, 0)
        up = jnp.where(local == 0, prev_row, up)
        dn = jnp.where(local == tm - 1, next_row, dn)
    up = jnp.where(pos == 0, 0.0, up)
    dn = jnp.where(pos == seq_len - 1, 0.0, dn)
    return w0 * up + w1 * u + w2 * dn


def _mod_kernel(c_ref, w_ref, b_ref, o_ref):
    s = _silu(c_ref[...]).astype(BF16)
    o_ref[...] = _dot(s, w_ref[...].astype(BF16)) + b_ref[...]


def _ada_mod(cvecs, ada_w, ada_b):
    nr = cvecs.shape[0]
    n = ada_w.shape[1]
    tn = 1024
    return pl.pallas_call(
        _mod_kernel,
        grid=(n // tn,),
        in_specs=[pl.BlockSpec((nr, D_MODEL), lambda j: (0, 0)),
                  pl.BlockSpec((D_MODEL, tn), lambda j: (0, j)),
                  pl.BlockSpec((1, tn), lambda j: (0, j))],
        out_specs=pl.BlockSpec((nr, tn), lambda j: (0, j)),
        out_shape=jax.ShapeDtypeStruct((nr, n), F32),
        compiler_params=_cparams("arbitrary"),
        name="ada_mod",
    )(cvecs, ada_w, ada_b.reshape(1, n))


def _prenorm_kernel(x_ref, g_ref, sc_ref, sh_ref, h_ref):
    y = _rms(x_ref[...]) * g_ref[...]
    h_ref[...] = (y * (1.0 + sc_ref[0]) + sh_ref[0]).astype(BF16)


def _prenorm(x, gain, sc, sh, seq_len):
    rows = x.shape[0]
    tm = 512
    per_b = max(seq_len // tm, 1)
    nb = sc.shape[0]
    bidx = (lambda i: (i // per_b, 0, 0)) if nb > 1 else (lambda i: (0, 0, 0))
    return pl.pallas_call(
        _prenorm_kernel,
        grid=(rows // tm,),
        in_specs=[pl.BlockSpec((tm, D_MODEL), lambda i: (i, 0)),
                  pl.BlockSpec((1, D_MODEL), lambda i: (0, 0)),
                  pl.BlockSpec((1, 1, D_MODEL), bidx),
                  pl.BlockSpec((1, 1, D_MODEL), bidx)],
        out_specs=pl.BlockSpec((tm, D_MODEL), lambda i: (i, 0)),
        out_shape=jax.ShapeDtypeStruct((rows, D_MODEL), BF16),
        compiler_params=_cparams("parallel"),
        name="prenorm",
    )(x, gain.reshape(1, D_MODEL), sc, sh)


def _hy_in_kernel(h_ref, w_ref, cw_ref, cb_ref, x0_ref, zz_ref, *, seq_len):
    h = h_ref[...]
    tm = h.shape[0]
    tc = x0_ref.shape[1]
    pos = lax.broadcasted_iota(jnp.int32, (tm, tc), 0) & (seq_len - 1)
    cw = cw_ref[...]
    cb = cb_ref[...]

    def branch(g):
        return _conv3(_dot(h, w_ref[g]), cw[0, g], cw[1, g], cw[2, g], pos, seq_len) + cb[g]

    x0_ref[...] = branch(0)
    zz_ref[...] = (branch(1) * branch(2)).astype(BF16)


def _hy_in(h, w3, cw, cb, seq_len):
    rows = h.shape[0]
    tm = max(seq_len, 1024)
    tc = 256
    return pl.pallas_call(
        functools.partial(_hy_in_kernel, seq_len=seq_len),
        grid=(rows // tm, D_A // tc),
        in_specs=[pl.BlockSpec((tm, D_MODEL), lambda i, j: (i, 0)),
                  pl.BlockSpec((3, D_MODEL, tc), lambda i, j: (0, 0, j)),
                  pl.BlockSpec((3, 3, tc), lambda i, j: (0, 0, j)),
                  pl.BlockSpec((3, tc), lambda i, j: (0, j))],
        out_specs=[pl.BlockSpec((tm, tc), lambda i, j: (i, j)),
                   pl.BlockSpec((tm, tc), lambda i, j: (i, j))],
        out_shape=[jax.ShapeDtypeStruct((rows, D_A), F32),
                   jax.ShapeDtypeStruct((rows, D_A), BF16)],
        compiler_params=_cparams("parallel", "arbitrary"),
        name="hyena_in",
    )(h, w3, cw, cb)


def _sc_kernel(h_ref, w_ref, cw_ref, y_ref, *, seq_len):
    h = h_ref[...]
    tm = h.shape[0]
    tc = y_ref.shape[1]
    pos = lax.broadcasted_iota(jnp.int32, (tm, tc), 0) & (seq_len - 1)
    cw = cw_ref[...]
    b_g = _dot(h, w_ref[0])
    cu = _dot(h, w_ref[1]) * _dot(h, w_ref[2])
    y_ref[...] = (b_g * _conv3(cu, cw[0], cw[1], cw[2], pos, seq_len)).astype(BF16)


def _short_conv(h, w3, cw, seq_len):
    rows = h.shape[0]
    tm = max(seq_len, 1024)
    tc = 256
    return pl.pallas_call(
        functools.partial(_sc_kernel, seq_len=seq_len),
        grid=(rows // tm, D_C // tc),
        in_specs=[pl.BlockSpec((tm, D_MODEL), lambda i, j: (i, 0)),
                  pl.BlockSpec((3, D_MODEL, tc), lambda i, j: (0, 0, j)),
                  pl.BlockSpec((3, tc), lambda i, j: (0, j))],
        out_specs=pl.BlockSpec((tm, tc), lambda i, j: (i, j)),
        out_shape=jax.ShapeDtypeStruct((rows, D_C), BF16),
        compiler_params=_cparams("parallel", "arbitrary"),
        name="short_conv",
    )(h, w3, cw)


def _filter_kernel(z_ref, w1_ref, b1_ref, w2_ref, b2_ref, fr_ref, w3_ref, b3_ref, tn_ref, dl_ref,
                   cf_ref, sm_ref, p_ref, q_ref, r_ref, e_scr, o_scr):
    fi = pl.program_id(1)
    seq_len = z_ref.shape[0]

    @pl.when(fi == 0)
    def _():
        fr = fr_ref[...]
        hdn = jnp.sin(fr[0:1] * (_dot_hi(z_ref[...], w1_ref[...]) + b1_ref[...]))
        hdn = jnp.sin(fr[1:2] * (_dot_hi(hdn, w2_ref[...]) + b2_ref[...]))
        window = jnp.exp(-tn_ref[...] * dl_ref[...])
        hfw = (_dot_hi(hdn, w3_ref[0]) + b3_ref[0:1]) * window
        hbw = (_dot_hi(hdn, w3_ref[1]) + b3_ref[1:2]) * window
        t = lax.broadcasted_iota(jnp.int32, hbw.shape, 0)
        hbw = jnp.where(t == 0, 0.0, hbw)
        inv = 1.0 / (jnp.sum(jnp.abs(hfw), axis=0, keepdims=True)
                     + jnp.sum(jnp.abs(hbw), axis=0, keepdims=True))
        e_scr[...] = (hfw + hbw) * inv
        o_scr[...] = (hfw - hbw) * inv

    e = e_scr[...]
    o = o_scr[...]
    e_hi = e.astype(BF16)
    e_lo = (e - e_hi.astype(F32)).astype(BF16)
    o_hi = o.astype(BF16)
    o_lo = (o - o_hi.astype(F32)).astype(BF16)
    cf = cf_ref[...]
    sm = sm_ref[...]
    kre = _dot(cf, e_hi) + _dot(cf, e_lo)
    kim = _dot(sm, o_hi) + _dot(sm, o_lo)
    t = lax.broadcasted_iota(jnp.int32, e.shape, 0)
    nyq = jnp.sum(jnp.where((t & 1) == 0, e, -e), axis=0, keepdims=True)
    f = lax.broadcasted_iota(jnp.int32, kre.shape, 0) + fi * kre.shape[0]
    row0 = f == 0
    p_ref[...] = kre
    q_ref[...] = jnp.where(row0, 0.0, kim)
    r_ref[...] = jnp.where(row0, nyq, kre)


def _filter_spectrum(fw, consts):
    seq_len = consts["z"].shape[0]
    tc = 512
    tf = min(seq_len, 512)
    full = lambda shape: pl.BlockSpec(shape, lambda j, f: (0,) * len(shape))
    spec_out = pl.BlockSpec((tf, tc), lambda j, f: (f, j))
    out = jax.ShapeDtypeStruct((seq_len, D_A), F32)
    return pl.pallas_call(
        _filter_kernel,
        grid=(D_A // tc, seq_len // tf),
        in_specs=[full((seq_len, FILTER_PAD)), full((FILTER_PAD, FILTER_PAD)), full((1, FILTER_PAD)),
                  full((FILTER_PAD, FILTER_PAD)), full((1, FILTER_PAD)), full((2, FILTER_PAD)),
                  pl.BlockSpec((2, FILTER_PAD, tc), lambda j, f: (0, 0, j)),
                  pl.BlockSpec((2, tc), lambda j, f: (0, j)),
                  full((seq_len, 1)),
                  pl.BlockSpec((1, tc), lambda j, f: (0, j)),
                  pl.BlockSpec((tf, seq_len), lambda j, f: (f, 0)),
                  pl.BlockSpec((tf, seq_len), lambda j, f: (f, 0))],
        out_specs=[spec_out, spec_out, spec_out],
        out_shape=[out, out, out],
        scratch_shapes=[pltpu.VMEM((seq_len, tc), F32), pltpu.VMEM((seq_len, tc), F32)],
        compiler_params=_cparams("parallel", "arbitrary"),
        name="hyena_filter",
    )(consts["z"], fw["w1"], fw["b1"], fw["w2"], fw["b2"], fw["freq"], fw["w3"], fw["b3"],
      consts["t_norm"], consts["deltas"], consts["cf"], consts["sm"])


def _dft_fwd_kernel(zz_ref, cf_ref, sm_ref, p_ref, q_ref, r_ref, zre_ref, zim_ref):
    u = zz_ref[0]
    ure = _dot(cf_ref[...], u)
    uim = _dot(sm_ref[...], u)
    p = p_ref[...]
    q = q_ref[...]
    zre_ref[0] = (ure * p - uim * q).astype(BF16)
    zim_ref[0] = (ure * q + uim * r_ref[...]).astype(BF16)


def _dft_fwd(zz, consts, spec):
    nb, seq_len, _ = zz.shape
    tc = 512
    tf = min(seq_len, 512)
    kspec = pl.BlockSpec((tf, tc), lambda b, j, f: (f, j))
    mspec = pl.BlockSpec((tf, seq_len), lambda b, j, f: (f, 0))
    ospec = pl.BlockSpec((1, tf, tc), lambda b, j, f: (b, f, j))
    out = jax.ShapeDtypeStruct((nb, seq_len, D_A), BF16)
    return pl.pallas_call(
        _dft_fwd_kernel,
        grid=(nb, D_A // tc, seq_len // tf),
        in_specs=[pl.BlockSpec((1, seq_len, tc), lambda b, j, f: (b, 0, j)),
                  mspec, mspec, kspec, kspec, kspec],
        out_specs=[ospec, ospec],
        out_shape=[out, out],
        compiler_params=_cparams("parallel", "parallel", "arbitrary"),
        name="hyena_dft_fwd",
    )(zz, consts["cf"], consts["sm"], *spec)


def _dft_inv_kernel(zre_ref, zim_ref, a_ref, b_ref, x0_ref, zz_ref, bias_ref, y_ref):
    y = _dot(a_ref[...], zre_ref[0]) + _dot(b_ref[...], zim_ref[0])
    y_ref[0] = (x0_ref[0] * (y + zz_ref[0].astype(F32) * bias_ref[...])).astype(BF16)


def _dft_inv(zre, zim, consts, x0, zz, bias):
    nb, seq_len, _ = zre.shape
    tc = 512
    tt = min(seq_len, 512)
    zspec = pl.BlockSpec((1, seq_len, tc), lambda b, j, t: (b, 0, j))
    mspec = pl.BlockSpec((tt, seq_len), lambda b, j, t: (t, 0))
    espec = pl.BlockSpec((1, tt, tc), lambda b, j, t: (b, t, j))
    return pl.pallas_call(
        _dft_inv_kernel,
        grid=(nb, D_A // tc, seq_len // tt),
        in_specs=[zspec, zspec, mspec, mspec, espec, espec,
                  pl.BlockSpec((1, tc), lambda b, j, t: (0, j))],
        out_specs=espec,
        out_shape=jax.ShapeDtypeStruct((nb, seq_len, D_A), BF16),
        compiler_params=_cparams("parallel", "parallel", "arbitrary"),
        name="hyena_dft_inv",
    )(zre, zim, consts["a"], consts["b"], x0, zz, bias.reshape(1, D_A))


def _mla_in_kernel(*refs, use_rope):
    if use_rope:
        (h_ref, w_ref, qn_ref, kn_ref, wq_ref, wk_ref, cq_ref, sq_ref, ck_ref, sk_ref,
         ckv_ref, kpe_ref, kc_ref, kp_ref, ql_ref, qp_ref) = refs
    else:
        (h_ref, w_ref, qn_ref, kn_ref, wq_ref, wk_ref,
         ckv_ref, kpe_ref, kc_ref, kp_ref, ql_ref, qp_ref) = refs
    p = _dot(h_ref[...], w_ref[...])
    cqn = (_rms(p[:, :Q_LORA]) * qn_ref[...]).astype(BF16)
    ckv = _rms(p[:, Q_LORA:Q_LORA + KV_LORA]) * kn_ref[...]
    ckv_ref[...] = ckv
    kc_ref[...] = ckv.astype(BF16)
    kpe = p[:, Q_LORA + KV_LORA:Q_LORA + KV_LORA + ROPE_DIM]
    kpe_ref[...] = kpe
    q = _dot(cqn, wq_ref[...])
    n_nope = N_HEADS * QK_NOPE
    n_pe = N_HEADS * ROPE_DIM
    q_pe = q[:, n_nope:n_nope + n_pe]
    if use_rope:
        kpe_rot = p[:, Q_LORA + KV_LORA + ROPE_DIM:Q_LORA + KV_LORA + 2 * ROPE_DIM]
        kp_ref[...] = (kpe * ck_ref[...] + kpe_rot * sk_ref[...]).astype(BF16)
        q_pe = q_pe * cq_ref[...] + q[:, n_nope + n_pe:n_nope + 2 * n_pe] * sq_ref[...]
    else:
        kp_ref[...] = kpe.astype(BF16)
    qp_ref[...] = (q_pe * ATTN_SCALE).astype(BF16)
    for hd in range(N_HEADS):
        qn = q[:, hd * QK_NOPE:(hd + 1) * QK_NOPE].astype(BF16)
        ql_ref[:, hd * KV_LORA:(hd + 1) * KV_LORA] = (_dot(qn, wk_ref[hd]) * ATTN_SCALE).astype(BF16)


def _mla_in(h, w_mla, q_norm, kv_norm, w_q, w_k, rope_tabs, seq_len):
    rows = h.shape[0]
    tm = min(seq_len, 512)
    use_rope = rope_tabs is not None
    const = lambda shape: pl.BlockSpec(shape, lambda i: (0,) * len(shape))
    rowb = lambda n: pl.BlockSpec((tm, n), lambda i: (i, 0))
    in_specs = [rowb(D_MODEL), const(w_mla.shape), const((1, Q_LORA)), const((1, KV_LORA)),
                const(w_q.shape), const(w_k.shape)]
    args = [h, w_mla, q_norm.reshape(1, Q_LORA), kv_norm.reshape(1, KV_LORA), w_q, w_k]
    if use_rope:
        per_b = seq_len // tm
        tab = lambda n: pl.BlockSpec((tm, n), lambda i: (i % per_b, 0))
        in_specs += [tab(N_HEADS * ROPE_DIM), tab(N_HEADS * ROPE_DIM), tab(ROPE_DIM), tab(ROPE_DIM)]
        args += [rope_tabs["cos_q"], rope_tabs["sin_q"], rope_tabs["cos_k"], rope_tabs["sin_k"]]
    widths = (KV_LORA, ROPE_DIM, KV_LORA, ROPE_DIM, N_HEADS * KV_LORA, N_HEADS * ROPE_DIM)
    dtypes = (F32, F32, BF16, BF16, BF16, BF16)
    return pl.pallas_call(
        functools.partial(_mla_in_kernel, use_rope=use_rope),
        grid=(rows // tm,),
        in_specs=in_specs,
        out_specs=[rowb(n) for n in widths],
        out_shape=[jax.ShapeDtypeStruct((rows, n), d) for n, d in zip(widths, dtypes)],
        compiler_params=_cparams("parallel"),
        name="mla_in",
    )(*args)


def _attn_kernel(ql_ref, qp_ref, kc_ref, kp_ref, wv_ref, y_ref):
    kc = kc_ref[0]
    kp = kp_ref[0]
    qp = qp_ref[...]
    for hd in range(N_HEADS):
        s = (_dot_nt(ql_ref[:, hd * KV_LORA:(hd + 1) * KV_LORA], kc)
             + _dot_nt(qp[:, hd * ROPE_DIM:(hd + 1) * ROPE_DIM], kp))
        e = jnp.exp(s - jnp.max(s, axis=-1, keepdims=True))
        inv = 1.0 / jnp.sum(e, axis=-1, keepdims=True)
        o = _dot(e.astype(BF16), kc) * inv
        y_ref[:, hd * V_DIM:(hd + 1) * V_DIM] = _dot(o.astype(BF16), wv_ref[hd]).astype(BF16)


def _attention(ql, qp, keys_c, keys_pe, w_v, seq_len):
    rows = ql.shape[0]
    nb, n_keys, _ = keys_c.shape
    tq = 256
    per_b = seq_len // tq
    rowb = lambda n: pl.BlockSpec((tq, n), lambda b, i: (b * per_b + i, 0))
    return pl.pallas_call(
        _attn_kernel,
        grid=(nb, per_b),
        in_specs=[rowb(N_HEADS * KV_LORA), rowb(N_HEADS * ROPE_DIM),
                  pl.BlockSpec((1, n_keys, KV_LORA), lambda b, i: (b, 0, 0)),
                  pl.BlockSpec((1, n_keys, ROPE_DIM), lambda b, i: (b, 0, 0)),
                  pl.BlockSpec(w_v.shape, lambda b, i: (0, 0, 0))],
        out_specs=rowb(N_HEADS * V_DIM),
        out_shape=jax.ShapeDtypeStruct((rows, N_HEADS * V_DIM), BF16),
        compiler_params=_cparams("parallel", "arbitrary"),
        name="mla_attention",
    )(ql, qp, keys_c, keys_pe, w_v)


def _merge_kernel(h_ref, ya_ref, yb_ref, yc_ref, wg_ref, wb_ref, m_ref):
    h = h_ref[...]
    acc = None
    for g, y_ref in enumerate((ya_ref, yb_ref, yc_ref)):
        term = jax.nn.sigmoid(_dot(h, wg_ref[g])) * _dot(y_ref[...], wb_ref[g])
        acc = term if acc is None else acc + term
    m_ref[...] = acc.astype(BF16)


def _merge(h, ya, yb, yc, w_gate, w_br):
    rows = h.shape[0]
    tm, tn = 512, 512
    rowb = lambda n: pl.BlockSpec((tm, n), lambda i, j: (i, 0))
    return pl.pallas_call(
        _merge_kernel,
        grid=(rows // tm, D_MODEL // tn),
        in_specs=[rowb(D_MODEL), rowb(D_A), rowb(D_A), rowb(D_A),
                  pl.BlockSpec((3, D_MODEL, tn), lambda i, j: (0, 0, j)),
                  pl.BlockSpec((3, D_A, tn), lambda i, j: (0, 0, j))],
        out_specs=pl.BlockSpec((tm, tn), lambda i, j: (i, j)),
        out_shape=jax.ShapeDtypeStruct((rows, D_MODEL), BF16),
        compiler_params=_cparams("parallel", "arbitrary"),
        name="branch_merge",
    )(h, ya, yb, yc, w_gate, w_br)


def _out_kernel(m_ref, wo_ref, x_ref, np_ref, g1_ref, nf_ref, sc_ref, sh_ref, x1_ref, h2_ref):
    o = _dot(m_ref[...], wo_ref[...])
    x1 = x_ref[...] + g1_ref[0] * (_rms(o) * np_ref[...])
    x1_ref[...] = x1
    h2_ref[...] = (_rms(x1) * nf_ref[...] * (1.0 + sc_ref[0]) + sh_ref[0]).astype(BF16)


def _out_proj(m, w_o, x, norm_post, g1, norm_ffn_pre, sc2, sh2, seq_len):
    rows = m.shape[0]
    tm = 256
    per_b = seq_len // tm
    nb = g1.shape[0]
    bidx = (lambda i: (i // per_b, 0, 0)) if nb > 1 else (lambda i: (0, 0, 0))
    rowb = pl.BlockSpec((tm, D_MODEL), lambda i: (i, 0))
    vec = pl.BlockSpec((1, D_MODEL), lambda i: (0, 0))
    modb = pl.BlockSpec((1, 1, D_MODEL), bidx)
    return pl.pallas_call(
        _out_kernel,
        grid=(rows // tm,),
        in_specs=[rowb, pl.BlockSpec((D_MODEL, D_MODEL), lambda i: (0, 0)), rowb,
                  vec, modb, vec, modb, modb],
        out_specs=[rowb, rowb],
        out_shape=[jax.ShapeDtypeStruct((rows, D_MODEL), F32),
                   jax.ShapeDtypeStruct((rows, D_MODEL), BF16)],
        compiler_params=_cparams("parallel"),
        name="out_proj",
    )(m, w_o, x, norm_post.reshape(1, D_MODEL), g1, norm_ffn_pre.reshape(1, D_MODEL), sc2, sh2)


def _ffn_kernel(h_ref, hp_ref, hn_ref, wu_ref, cw_ref, cb_ref, wd_ref, x1_ref, g2_ref, np_ref,
                x2_ref, acc_ref, *, seq_len):
    i = pl.program_id(0)
    j = pl.program_id(1)
    tm = h_ref.shape[0]
    tf = wd_ref.shape[0]

    @pl.when(j == 0)
    def _():
        acc_ref[...] = jnp.zeros_like(acc_ref)

    h = h_ref[...]
    hp = hp_ref[...]
    hn = hn_ref[...]
    pos = (lax.broadcasted_iota(jnp.int32, (tm, tf), 0) + i * tm) & (seq_len - 1)
    cw = cw_ref[...]
    cb = cb_ref[...]
    halves = []
    for g in range(2):
        w = wu_ref[g]
        u = _dot(h, w)
        prev_row = _dot(hp, w)[BF16_SUBLANES - 1:BF16_SUBLANES]
        next_row = _dot(hn, w)[0:1]
        halves.append(_conv3(u, cw[0, g], cw[1, g], cw[2, g], pos, seq_len, prev_row, next_row) + cb[g])
    act = (_silu(halves[0]) * halves[1]).astype(BF16)
    acc_ref[...] += _dot(act, wd_ref[...])

    @pl.when(j == pl.num_programs(1) - 1)
    def _():
        x2_ref[...] = x1_ref[...] + g2_ref[0] * (_rms(acc_ref[...]) * np_ref[...])


def _conv_ffn(h2, w_up, cw, cb, w_down, x1, g2, norm_post, seq_len):
    rows = h2.shape[0]
    tm, tf = 512, 512
    per_b = max(seq_len // tm, 1)
    nb = g2.shape[0]
    bidx = (lambda i, j: (i // per_b, 0, 0)) if nb > 1 else (lambda i, j: (0, 0, 0))
    hb = tm // BF16_SUBLANES
    last = rows // BF16_SUBLANES - 1
    rowb = pl.BlockSpec((tm, D_MODEL), lambda i, j: (i, 0))
    return pl.pallas_call(
        functools.partial(_ffn_kernel, seq_len=seq_len),
        grid=(rows // tm, D_FF // tf),
        in_specs=[rowb,
                  pl.BlockSpec((BF16_SUBLANES, D_MODEL), lambda i, j: (jnp.maximum(i * hb - 1, 0), 0)),
                  pl.BlockSpec((BF16_SUBLANES, D_MODEL), lambda i, j: (jnp.minimum((i + 1) * hb, last), 0)),
                  pl.BlockSpec((2, D_MODEL, tf), lambda i, j: (0, 0, j)),
                  pl.BlockSpec((3, 2, tf), lambda i, j: (0, 0, j)),
                  pl.BlockSpec((2, tf), lambda i, j: (0, j)),
                  pl.BlockSpec((tf, D_MODEL), lambda i, j: (j, 0)),
                  rowb,
                  pl.BlockSpec((1, 1, D_MODEL), bidx),
                  pl.BlockSpec((1, D_MODEL), lambda i, j: (0, 0))],
        out_specs=rowb,
        out_shape=jax.ShapeDtypeStruct((rows, D_MODEL), F32),
        scratch_shapes=[pltpu.VMEM((tm, D_MODEL), F32)],
        compiler_params=_cparams("parallel", "arbitrary"),
        name="conv_ffn",
    )(h2, h2, h2, w_up, cw, cb, w_down, x1, g2, norm_post.reshape(1, D_MODEL))


def _dft_consts(seq_len):
    n = 2 * seq_len
    idx = jnp.arange(seq_len, dtype=jnp.int32)
    ang = (2.0 * math.pi / n) * ((idx[:, None] * idx[None, :]) % n).astype(F32)
    cos, sin = jnp.cos(ang), jnp.sin(ang)
    alt = jnp.where(idx % 2 == 0, 1.0, -1.0).astype(F32)
    row0 = (idx == 0)[:, None]
    sm = jnp.where(row0, alt[None, :], -sin)
    a = jnp.where(row0.T, 1.0 / n, (2.0 / n) * cos)
    b = jnp.where(row0.T, alt[:, None] / n, (-2.0 / n) * sin)
    t_idx = jnp.arange(seq_len, dtype=F32)
    t_norm = t_idx / max(seq_len - 1, 1)
    w = 2.0 * math.pi * t_idx / seq_len
    bands = jnp.linspace(1e-4, N_BANDS - 1, N_BANDS, dtype=F32)
    wb = w[:, None] * bands[None, :]
    z = jnp.concatenate([t_norm[:, None], jnp.cos(wb), -jnp.sin(wb)], axis=-1)
    z = jnp.pad(z, ((0, 0), (0, FILTER_PAD - EMB_DIM)))
    deltas = jnp.abs(jnp.linspace(MIN_DECAY, MAX_DECAY, D_A, dtype=F32))
    return dict(cf=cos.astype(BF16), sm=sm.astype(BF16), a=a.astype(BF16), b=b.astype(BF16),
                z=z, t_norm=t_norm[:, None], deltas=deltas[None, :])


def _rope_tables(seq_len):
    rows = seq_len // GRID_W
    row = jnp.repeat(jnp.arange(rows, dtype=F32), GRID_W)
    col = jnp.tile(jnp.arange(GRID_W, dtype=F32), rows)
    inv = ROPE_THETA ** (-jnp.arange(AXIS_PAIRS, dtype=F32) / AXIS_PAIRS)
    ang = jnp.concatenate([row[:, None] * inv, col[:, None] * inv], axis=-1)
    cos = jnp.concatenate([jnp.cos(ang)] * 2, axis=-1)
    sin = jnp.concatenate([jnp.sin(ang)] * 2, axis=-1)
    return dict(cos_k=cos, sin_k=sin, cos_q=jnp.tile(cos, (1, N_HEADS)), sin_q=jnp.tile(sin, (1, N_HEADS)))


def _rot_cols(w):
    return jnp.concatenate([-w[..., ROPE_HALF:], w[..., :ROPE_HALF]], axis=-1)


def _layer_weights(l, p):
    w_in = p["w_in"][l].astype(BF16)
    c0 = 3 * D_A
    c1 = c0 + Q_LORA + KV_LORA + ROPE_DIM
    c2 = c1 + 3 * D_C
    split3 = lambda w, n: jnp.stack([w[:, k * n:(k + 1) * n] for k in range(3)])
    w_mla = w_in[:, c0:c1]
    w_kpe = w_mla[:, Q_LORA + KV_LORA:]
    w_uq = p["w_uq"][l].astype(BF16).reshape(Q_LORA, N_HEADS, QK_NOPE + ROPE_DIM)
    w_q_nope = w_uq[..., :QK_NOPE].reshape(Q_LORA, N_HEADS * QK_NOPE)
    w_q_pe = w_uq[..., QK_NOPE:]
    w_ukv = p["w_ukv"][l].astype(BF16).reshape(KV_LORA, N_HEADS, QK_NOPE + V_DIM)
    pad = lambda a, r, c: jnp.pad(a, ((0, r - a.shape[0]), (0, c - a.shape[1])))
    w3 = pad(p["hy_f_w3"][l], FILTER_PAD, 2 * D_A)
    return dict(
        w_hy=split3(w_in[:, :c0], D_A),
        hy_cw=p["hy_conv_w"][l].reshape(3, 3, D_A),
        hy_cb=p["hy_conv_b"][l].reshape(3, D_A),
        w_mla=jnp.concatenate([w_mla, _rot_cols(w_kpe)], axis=1),
        w_q=jnp.concatenate([w_q_nope, w_q_pe.reshape(Q_LORA, -1)], axis=1),
        w_q_rope=jnp.concatenate([w_q_nope, w_q_pe.reshape(Q_LORA, -1),
                                  _rot_cols(w_q_pe).reshape(Q_LORA, -1)], axis=1),
        w_k=jnp.transpose(w_ukv[..., :QK_NOPE], (1, 2, 0)),
        w_v=jnp.transpose(w_ukv[..., QK_NOPE:], (1, 0, 2)),
        w_sc=split3(w_in[:, c1:c2], D_C),
        w_gate=split3(w_in[:, c2:], D_MODEL),
        w_br=jnp.stack([p["w_br_a"][l], p["w_br_b"][l], p["w_br_c"][l]]).astype(BF16),
        w_o=p["w_o"][l].astype(BF16),
        w_up=jnp.stack([p["ffn_up"][l][:, :D_FF], p["ffn_up"][l][:, D_FF:]]).astype(BF16),
        ffn_cw=p["ffn_conv_w"][l].reshape(3, 2, D_FF),
        ffn_cb=p["ffn_conv_b"][l].reshape(2, D_FF),
        w_down=p["ffn_down"][l].astype(BF16),
        filt=dict(
            w1=pad(p["hy_f_w1"][l], FILTER_PAD, FILTER_PAD),
            b1=pad(p["hy_f_b1"][l][None], 1, FILTER_PAD),
            w2=pad(p["hy_f_w2"][l], FILTER_PAD, FILTER_PAD),
            b2=pad(p["hy_f_b2"][l][None], 1, FILTER_PAD),
            freq=pad(p["hy_f_freq"][l], 2, FILTER_PAD),
            w3=jnp.stack([w3[:, :D_A], w3[:, D_A:]]),
            b3=p["hy_f_b3"][l].reshape(2, D_A)),
    )


def _trunk_layer(x, mods, wl, p, l, nb, seq_len, consts, rope_tabs, ctx_kv):
    sh1, sc1, g1, sh2, sc2, g2 = mods
    h = _prenorm(x, p["norm_mix_pre"][l], sc1, sh1, seq_len)
    x0, zz = _hy_in(h, wl["w_hy"], wl["hy_cw"], wl["hy_cb"], seq_len)
    spec = _filter_spectrum(wl["filt"], consts)
    zz3 = zz.reshape(nb, seq_len, D_A)
    zre, zim = _dft_fwd(zz3, consts, spec)
    ya = _dft_inv(zre, zim, consts, x0.reshape(nb, seq_len, D_A), zz3, p["hy_bias"][l])
    ya = ya.reshape(nb * seq_len, D_A)
    use_rope = rope_tabs is not None
    ckv, kpe, kc, kp, ql, qp = _mla_in(
        h, wl["w_mla"], p["q_norm"][l], p["kv_norm"][l],
        wl["w_q_rope"] if use_rope else wl["w_q"], wl["w_k"], rope_tabs, seq_len)
    keys_c = kc.reshape(nb, seq_len, KV_LORA)
    keys_pe = kp.reshape(nb, seq_len, ROPE_DIM)
    if ctx_kv is not None:
        keys_c = jnp.concatenate([keys_c, ctx_kv[0].astype(BF16)], axis=1)
        keys_pe = jnp.concatenate([keys_pe, ctx_kv[1].astype(BF16)], axis=1)
    yb = _attention(ql, qp, keys_c, keys_pe, wl["w_v"], seq_len)
    yc = _short_conv(h, wl["w_sc"], p["sc_conv_w"][l], seq_len)
    m = _merge(h, ya, yb, yc, wl["w_gate"], wl["w_br"])
    x1, h2 = _out_proj(m, wl["w_o"], x, p["norm_mix_post"][l], g1, p["norm_ffn_pre"][l], sc2, sh2, seq_len)
    x2 = _conv_ffn(h2, wl["w_up"], wl["ffn_cw"], wl["ffn_cb"], wl["w_down"], x1, g2,
                   p["norm_ffn_post"][l], seq_len)
    return x2, ckv, kpe


def kernel(x_prompt, x_sample, c, cache_ckv, cache_kpe, c_ctx, ada_w, ada_b, norm_mix_pre, norm_mix_post, norm_ffn_pre, norm_ffn_post, w_in, hy_conv_w, hy_conv_b, hy_f_w1, hy_f_b1, hy_f_w2, hy_f_b2, hy_f_w3, hy_f_b3, hy_f_freq, hy_bias, q_norm, kv_norm, w_uq, w_ukv, sc_conv_w, w_br_a, w_br_b, w_br_c, w_o, ffn_up, ffn_conv_w, ffn_conv_b, ffn_down):
    p = dict(norm_mix_pre=norm_mix_pre, norm_mix_post=norm_mix_post, norm_ffn_pre=norm_ffn_pre,
             norm_ffn_post=norm_ffn_post, w_in=w_in, hy_conv_w=hy_conv_w, hy_conv_b=hy_conv_b,
             hy_f_w1=hy_f_w1, hy_f_b1=hy_f_b1, hy_f_w2=hy_f_w2, hy_f_b2=hy_f_b2, hy_f_w3=hy_f_w3,
             hy_f_b3=hy_f_b3, hy_f_freq=hy_f_freq, hy_bias=hy_bias, q_norm=q_norm, kv_norm=kv_norm,
             w_uq=w_uq, w_ukv=w_ukv, sc_conv_w=sc_conv_w, w_br_a=w_br_a, w_br_b=w_br_b,
             w_br_c=w_br_c, w_o=w_o, ffn_up=ffn_up, ffn_conv_w=ffn_conv_w, ffn_conv_b=ffn_conv_b,
             ffn_down=ffn_down)
    n_ctx, ctx_len, _ = x_prompt.shape
    n_lat, lat_len, _ = x_sample.shape
    ctx_consts = _dft_consts(ctx_len)
    lat_consts = _dft_consts(lat_len)
    rope_tabs = _rope_tables(lat_len)
    mod_rows = 16
    cvecs = jnp.concatenate(
        [c_ctx[None, :], c, jnp.zeros((mod_rows - 1 - n_lat, D_MODEL), F32)], axis=0)
    xp = x_prompt.reshape(n_ctx * ctx_len, D_MODEL)
    xs = x_sample.reshape(n_lat * lat_len, D_MODEL)
    ckv_list, kpe_list = [], []
    for l in range(DEPTH):
        wl = _layer_weights(l, p)
        mod = _ada_mod(cvecs, ada_w[l], ada_b[l])
        ctx_mods = [mod[0:1, k * D_MODEL:(k + 1) * D_MODEL].reshape(1, 1, D_MODEL) for k in range(6)]
        lat_mods = [mod[1:1 + n_lat, k * D_MODEL:(k + 1) * D_MODEL].reshape(n_lat, 1, D_MODEL)
                    for k in range(6)]
        xp, ckv_l, kpe_l = _trunk_layer(xp, ctx_mods, wl, p, l, n_ctx, ctx_len, ctx_consts, None, None)
        ckv_list.append(ckv_l.reshape(n_ctx, ctx_len, KV_LORA))
        kpe_list.append(kpe_l.reshape(n_ctx, ctx_len, ROPE_DIM))
        xs, _, _ = _trunk_layer(xs, lat_mods, wl, p, l, n_lat, lat_len, lat_consts, rope_tabs,
                                (cache_ckv[:, l], cache_kpe[:, l]))
    return (xp.reshape(n_ctx, ctx_len, D_MODEL), xs.reshape(n_lat, lat_len, D_MODEL),
            jnp.stack(ckv_list, axis=1), jnp.stack(kpe_list, axis=1))
```
